```python
import math
import jax, jax.numpy as jnp
from jax import lax
import numpy as np

D_MODEL = 4096
BATCH = 4
SEQ = 4096
DEPTH = 4

GRID_W = 64
CTX_LEN = 256
DN_HEADS = 16
DN_HEAD_DIM = 128
DN_W = DN_HEADS * DN_HEAD_DIM
CONF_W = 2048
MIX_W = DN_W + CONF_W
SHORT_CONV = 5
DN_CHUNK = 64
CONF_K = 31
IN_W = 4 * DN_W + 4 * DN_HEADS + 2 * CONF_W
CONF_OFF = 4 * DN_W + 4 * DN_HEADS
N_EXPERTS = 16
EXPERT_FF = 256
CAP_FACTOR = 2
EPS = 1e-6

kernel_name = "hymba_deltanet_conformer_ecmoe_dit"


def rmsnorm(x):
    xf = x.astype(jnp.float32)
    return (xf * lax.rsqrt(jnp.mean(xf * xf, axis=-1, keepdims=True) + EPS)).astype(x.dtype)


def layernorm(x, g, b):
    xf = x.astype(jnp.float32)
    mu = jnp.mean(xf, axis=-1, keepdims=True)
    var = jnp.mean(jnp.square(xf - mu), axis=-1, keepdims=True)
    return ((xf - mu) * lax.rsqrt(var + EPS) * g + b).astype(x.dtype)


def l2norm(x):
    return x * lax.rsqrt(jnp.sum(x * x, axis=-1, keepdims=True) + EPS)


def adaln(cond, w, b):
    m = (jax.nn.silu(cond) @ w + b).reshape(-1, 6, D_MODEL)
    return jnp.moveaxis(m, 1, 0)[:, :, None, :]


def modulate(h, shift, scale):
    return h * (1.0 + scale) + shift


def depthwise_conv1d(x, w):
    k, ch = w.shape
    pad = k // 2
    return lax.conv_general_dilated(x, w.reshape(k, 1, ch), (1,), ((pad, pad),),
                                    dimension_numbers=("NWC", "WIO", "NWC"), feature_group_count=ch)


def depthwise_conv_grid(x, w, axis):
    k, ch = w.shape
    pad = k // 2
    if axis == 1:
        kern, padding = w.reshape(k, 1, 1, ch), ((pad, pad), (0, 0))
    else:
        kern, padding = w.reshape(1, k, 1, ch), ((0, 0), (pad, pad))
    return lax.conv_general_dilated(x, kern, (1, 1), padding,
                                    dimension_numbers=("NHWC", "HWIO", "NHWC"), feature_group_count=ch)


def chunk_gated_delta(q, k, v, g, beta, s0):
    f32 = jnp.float32
    bsz, t_len, h, dk = q.shape
    dv = v.shape[-1]
    n = t_len // DN_CHUNK
    q = l2norm(q.astype(f32)) * (dk ** -0.5)
    k = l2norm(k.astype(f32))

    def to_chunks(t):
        return jnp.moveaxis(t.reshape((bsz, n, DN_CHUNK, h) + t.shape[3:]), 3, 1)

    qc, kc, vc = to_chunks(q), to_chunks(k), to_chunks(v.astype(f32))
    gc = jnp.cumsum(to_chunks(g.astype(f32)), axis=-1)
    bc = to_chunks(beta.astype(f32))
    tri = jnp.tril(jnp.ones((DN_CHUNK, DN_CHUNK), dtype=bool))
    strict = jnp.tril(jnp.ones((DN_CHUNK, DN_CHUNK), dtype=bool), k=-1)
    decay = jnp.exp(jnp.where(tri, gc[..., :, None] - gc[..., None, :], -jnp.inf))
    a_mat = jnp.where(strict, jnp.einsum("bhnid,bhnjd->bhnij", kc, kc) * decay * bc[..., :, None], 0.0)
    lhs = a_mat + jnp.eye(DN_CHUNK, dtype=f32)
    rhs = jnp.concatenate([vc * bc[..., None], kc * (bc * jnp.exp(gc))[..., None]], axis=-1)
    sol = lax.linalg.triangular_solve(lhs, rhs, left_side=True, lower=True, unit_diagonal=True)
    u, w = sol[..., :dv], sol[..., dv:]
    qk = jnp.where(tri, jnp.einsum("bhnid,bhnjd->bhnij", qc, kc) * decay, 0.0)
    q_dec = qc * jnp.exp(gc)[..., None]
    g_last = gc[..., -1]
    k_dec = kc * jnp.exp(g_last[..., None] - gc)[..., None]
    xs = tuple(jnp.moveaxis(t, 2, 0) for t in (w, u, qk, q_dec, k_dec, g_last))

    def step(s, inp):
        w_i, u_i, qk_i, qd_i, kd_i, gl_i = inp
        v_new = u_i - jnp.einsum("bhck,bhkv->bhcv", w_i, s)
        o_i = jnp.einsum("bhck,bhkv->bhcv", qd_i, s) + jnp.einsum("bhij,bhjv->bhiv", qk_i, v_new)
        s = s * jnp.exp(gl_i)[..., None, None] + jnp.einsum("bhck,bhcv->bhkv", kd_i, v_new)
        return s, o_i

    s_fin, o = lax.scan(step, s0.astype(f32), xs)
    o = jnp.moveaxis(jnp.moveaxis(o, 0, 2), 1, 3).reshape(bsz, t_len, h, dv)
    return o, s_fin


def deltanet_group(p, conv_w, a_log, dt_bias, norm_w, s0):
    bsz, t_len, _ = p.shape
    qkv = jax.nn.silu(depthwise_conv1d(p[..., :3 * DN_W], conv_w))
    q, k, v = [t.reshape(bsz, t_len, DN_HEADS, DN_HEAD_DIM) for t in jnp.split(qkv, 3, axis=-1)]
    z = p[..., 3 * DN_W:4 * DN_W].reshape(bsz, t_len, DN_HEADS, DN_HEAD_DIM)
    ba = p[..., 4 * DN_W:CONF_OFF].astype(jnp.float32).reshape(bsz, t_len, 4, DN_HEADS)
    beta = jax.nn.sigmoid(ba[:, :, :2])
    g = -jnp.exp(a_log.astype(jnp.float32)) * jax.nn.softplus(ba[:, :, 2:] + dt_bias.astype(jnp.float32))
    flip = lambda t: jnp.flip(t, axis=1)
    o_f, s_f = chunk_gated_delta(q, k, v, g[:, :, 0], beta[:, :, 0], s0[0])
    o_b, s_b = chunk_gated_delta(flip(q), flip(k), flip(v), flip(g[:, :, 1]), flip(beta[:, :, 1]), s0[1])
    o = o_f + flip(o_b)
    o = rmsnorm(o) * norm_w.astype(jnp.float32) * jax.nn.silu(z.astype(jnp.float32))
    return o.reshape(bsz, t_len, DN_W).astype(p.dtype), jnp.stack([s_f, s_b])


def conformer_group(p, dw, dw_b, ln_g, ln_b, on_grid):
    a = p[..., CONF_OFF:CONF_OFF + CONF_W]
    gate = p[..., CONF_OFF + CONF_W:]
    u = a * jax.nn.sigmoid(gate)
    if on_grid:
        bsz, t_len, ch = u.shape
        rows = t_len // GRID_W
        ug = u.reshape(bsz, rows, GRID_W, ch)
        half = ch // 2
        y_h = depthwise_conv_grid(ug[..., :half], dw[:, :half], axis=2)
        y_v = depthwise_conv_grid(ug[..., half:], dw[:, half:], axis=1)
        y = jnp.concatenate([y_h, y_v], axis=-1).reshape(bsz, t_len, ch)
    else:
        y = depthwise_conv1d(u, dw)
    return jax.nn.silu(layernorm(y + dw_b, ln_g, ln_b))


def expert_choice_ffn(h, w_router, w_gate, w_up, w_down):
    bsz, t_len, _ = h.shape
    cap = (CAP_FACTOR * t_len) // N_EXPERTS
    aff = jax.nn.softmax(jnp.einsum("btd,de->bte", h.astype(jnp.float32), w_router.astype(jnp.float32)), axis=-1)
    g, idx = lax.top_k(jnp.swapaxes(aff, 1, 2), cap)
    bidx = jnp.arange(bsz)[:, None, None]
    xs = h[bidx, idx]
    a = jnp.einsum("becd,edf->becf", xs, w_gate)
    up = jnp.einsum("becd,edf->becf", xs, w_up)
    y = jnp.einsum("becf,efd->becd", jax.nn.silu(a) * up, w_down) * g[..., None].astype(h.dtype)
    return jnp.zeros_like(h).at[bidx, idx].add(y)


def setup_inputs(seed: int = 0) -> dict:
    key = jax.random.key(seed)
    ks = jax.random.split(key, 24)
    f32 = jnp.float32
    nrm = lambda k, shape, s: jax.random.normal(k, shape, f32) * s
    dt = jnp.exp(jax.random.uniform(ks[9], (DEPTH, 2, DN_HEADS), f32, math.log(1e-3), math.log(1e-1)))
    return {
        "x": nrm(ks[0], (BATCH, SEQ, D_MODEL), 1.0),
        "c": nrm(ks[1], (BATCH, D_MODEL), 1.0),
        "ctx": nrm(ks[2], (BATCH, CTX_LEN, D_MODEL), 1.0),
        "c_ctx": nrm(ks[3], (D_MODEL,), 1.0),
        "w_ada": nrm(ks[4], (DEPTH, D_MODEL, 6 * D_MODEL), 0.5 * D_MODEL ** -0.5),
        "b_ada": nrm(ks[5], (DEPTH, 6 * D_MODEL), 0.02),
        "w_in": nrm(ks[6], (DEPTH, D_MODEL, IN_W), D_MODEL ** -0.5),
        "conv_qkv": nrm(ks[7], (DEPTH, SHORT_CONV, 3 * DN_W), SHORT_CONV ** -0.5),
        "a_log": jnp.log(jax.random.uniform(ks[8], (DEPTH, 2, DN_HEADS), f32, 1.0, 16.0)),
        "dt_bias": dt + jnp.log(-jnp.expm1(-dt)),
        "dn_norm": 1.0 + nrm(ks[10], (DEPTH, DN_HEAD_DIM), 0.02),
        "conf_dw": nrm(ks[11], (DEPTH, CONF_K, CONF_W), CONF_K ** -0.5),
        "conf_dw_b": nrm(ks[12], (DEPTH, CONF_W), 0.02),
        "conf_ln_g": 1.0 + nrm(ks[13], (DEPTH, CONF_W), 0.02),
        "conf_ln_b": nrm(ks[14], (DEPTH, CONF_W), 0.02),
        "w_out": nrm(ks[15], (DEPTH, MIX_W, D_MODEL), MIX_W ** -0.5),
        "w_router": nrm(ks[16], (DEPTH, D_MODEL, N_EXPERTS), D_MODEL ** -0.5),
        "w_gate_e": nrm(ks[17], (DEPTH, N_EXPERTS, D_MODEL, EXPERT_FF), D_MODEL ** -0.5),
        "w_up_e": nrm(ks[18], (DEPTH, N_EXPERTS, D_MODEL, EXPERT_FF), D_MODEL ** -0.5),
        "w_down_e": nrm(ks[19], (DEPTH, N_EXPERTS, EXPERT_FF, D_MODEL), EXPERT_FF ** -0.5),
        "final_norm": 1.0 + nrm(ks[20], (D_MODEL,), 0.02),
    }


def reference(x, c, ctx, c_ctx, w_ada, b_ada, w_in, conv_qkv, a_log, dt_bias, dn_norm, conf_dw,
              conf_dw_b, conf_ln_g, conf_ln_b, w_out, w_router, w_gate_e, w_up_e, w_down_e, final_norm):
    bsz = x.shape[0]
    x_lat, x_ctx = x, ctx
    zero_state = jnp.zeros((2, bsz, DN_HEADS, DN_HEAD_DIM, DN_HEAD_DIM), jnp.float32)
    for l in range(DEPTH):
        last = l == DEPTH - 1
        ml = adaln(c, w_ada[l], b_ada[l])
        mc = adaln(c_ctx[None], w_ada[l], b_ada[l])
        p_lat = modulate(rmsnorm(x_lat), ml[0], ml[1]) @ w_in[l]
        p_ctx = modulate(rmsnorm(x_ctx), mc[0], mc[1]) @ w_in[l]
        dn_ctx, ctx_states = deltanet_group(p_ctx, conv_qkv[l], a_log[l], dt_bias[l], dn_norm[l], zero_state)
        dn_lat, _ = deltanet_group(p_lat, conv_qkv[l], a_log[l], dt_bias[l], dn_norm[l], ctx_states)
        cf_lat = conformer_group(p_lat, conf_dw[l], conf_dw_b[l], conf_ln_g[l], conf_ln_b[l], True)
        y_lat = jnp.concatenate([dn_lat, cf_lat], axis=-1) @ w_out[l]
        x_lat = x_lat + ml[2] * y_lat
        h_lat = modulate(rmsnorm(x_lat), ml[3], ml[4])
        x_lat = x_lat + ml[5] * expert_choice_ffn(h_lat, w_router[l], w_gate_e[l], w_up_e[l], w_down_e[l])
        if not last:
            cf_ctx = conformer_group(p_ctx, conf_dw[l], conf_dw_b[l], conf_ln_g[l], conf_ln_b[l], False)
            y_ctx = jnp.concatenate([dn_ctx, cf_ctx], axis=-1) @ w_out[l]
            x_ctx = x_ctx + mc[2] * y_ctx
            h_ctx = modulate(rmsnorm(x_ctx), mc[3], mc[4])
            x_ctx = x_ctx + mc[5] * expert_choice_ffn(h_ctx, w_router[l], w_gate_e[l], w_up_e[l], w_down_e[l])
    return rmsnorm(x_lat) * final_norm
```

```python
import functools
import math

import numpy as np
import jax
import jax.numpy as jnp
from jax import lax
from jax.experimental import pallas as pl
from jax.experimental.pallas import tpu as pltpu

F32 = jnp.float32
BF16 = jnp.bfloat16

EPS = 1e-6
GRID_W = 64
CAP_FACTOR = 2
DN_CHUNK = 64
LANES = 128
SUBLANES = 8
PAD_ROWS = 16
VMEM_LIMIT_BYTES = 56 * 1024 * 1024


def _cparams(*sem):
    return pltpu.CompilerParams(dimension_semantics=sem, vmem_limit_bytes=VMEM_LIMIT_BYTES)


def _pick(n, cands):
    for c in cands:
        if n % c == 0:
            return c
    return n


def _sigmoid(x):
    return 1.0 / (1.0 + jnp.exp(-x))


def _dot(a, b):
    return jnp.dot(a, b, preferred_element_type=F32)


def _dot_nt(a, b):
    return lax.dot_general(a, b, (((1,), (1,)), ((), ())), preferred_element_type=F32)


def _dot_tn(a, b):
    return lax.dot_general(a, b, (((0,), (0,)), ((), ())), preferred_element_type=F32)


def _split3(x):
    x1 = x.astype(BF16)
    r1 = x - x1.astype(F32)
    x2 = r1.astype(BF16)
    r2 = r1 - x2.astype(F32)
    return x1, x2, r2.astype(BF16)


def _ada_body(c_ref, w_ref, b_ref, o_ref):
    c = c_ref[...]
    s = (c * _sigmoid(c)).astype(BF16)
    o_ref[0] = _dot(s, w_ref[0].astype(BF16)) + b_ref[0]


def _adaln(cond, w_ada, b_ada):
    depth, d, n6 = w_ada.shape
    nb = cond.shape[0]
    tn = _pick(n6, (512, 256, 128))
    return pl.pallas_call(
        _ada_body,
        grid=(depth, n6 // tn),
        in_specs=[pl.BlockSpec((nb, d), lambda l, j: (0, 0)),
                  pl.BlockSpec((1, d, tn), lambda l, j: (l, 0, j)),
                  pl.BlockSpec((1, 1, tn), lambda l, j: (l, 0, j))],
        out_specs=pl.BlockSpec((1, nb, tn), lambda l, j: (l, 0, j)),
        out_shape=jax.ShapeDtypeStruct((depth, nb, n6), F32),
        compiler_params=_cparams("parallel", "parallel"),
        name="adaln",
    )(cond, w_ada, b_ada.reshape(depth, 1, n6))


def _modnorm(x, shift, scale):
    ms = jnp.mean(x * x, axis=-1, keepdims=True)
    return (x * lax.rsqrt(ms + EPS)) * (1.0 + scale) + shift


def _inproj_body(x_ref, mod_ref, w_ref, wba_ref, o_ref, oba_ref, h_scr):
    @pl.when(pl.program_id(1) == 0)
    def _():
        h = _modnorm(x_ref[...], mod_ref[0, 0:1, :], mod_ref[0, 1:2, :]).astype(BF16)
        h_scr[...] = h
        oba_ref[...] = _dot(h, wba_ref[...])

    o_ref[...] = _dot(h_scr[...], w_ref[...])


def _inproj(x2, mod, w_main, w_ba, t_len):
    m, d = x2.shape
    n = w_main.shape[1]
    per_batch = mod.shape[0] > 1
    tm = min(512, t_len if per_batch else m)
    tn = _pick(n, (1024, 512, 256, 128))
    bidx = (lambda i: (i * tm) // t_len) if per_batch else (lambda i: 0)
    return pl.pallas_call(
        _inproj_body,
        grid=(m // tm, n // tn),
        in_specs=[pl.BlockSpec((tm, d), lambda i, j: (i, 0)),
                  pl.BlockSpec((1, 6, d), lambda i, j: (bidx(i), 0, 0)),
                  pl.BlockSpec((d, tn), lambda i, j: (0, j)),
                  pl.BlockSpec((d, LANES), lambda i, j: (0, 0))],
        out_specs=[pl.BlockSpec((tm, tn), lambda i, j: (i, j)),
                   pl.BlockSpec((tm, LANES), lambda i, j: (i, 0))],
        out_shape=[jax.ShapeDtypeStruct((m, n), F32), jax.ShapeDtypeStruct((m, LANES), F32)],
        scratch_shapes=[pltpu.VMEM((tm, d), BF16)],
        compiler_params=_cparams("parallel", "arbitrary"),
        name="inproj",
    )(x2, mod, w_main, w_ba)


def _gates_body(ba_ref, gp_ref, o_ref, *, heads, chunk):
    tm = ba_ref.shape[0]
    ba = ba_ref[...]
    xs = ba + gp_ref[1:2, :]
    softplus = jnp.maximum(xs, 0.0) + jnp.log1p(jnp.exp(-jnp.abs(xs)))
    g = -jnp.exp(gp_ref[0:1, :]) * softplus
    g1, g2, g3 = _split3(g)
    gs = jnp.concatenate([g1, g2, g3], axis=1)
    ii = lax.broadcasted_iota(jnp.int32, (tm, tm), 0)
    jj = lax.broadcasted_iota(jnp.int32, (tm, tm), 1)
    sh = int(math.log2(chunk))
    same = jnp.right_shift(ii, sh) == jnp.right_shift(jj, sh)
    lo = jnp.where(same & (jj <= ii), 1.0, 0.0).astype(BF16)
    up = jnp.where(same & (jj >= ii), 1.0, 0.0).astype(BF16)
    pf = _dot(lo, gs)
    sf = _dot(up, gs)
    pf = pf[:, :LANES] + pf[:, LANES:2 * LANES] + pf[:, 2 * LANES:]
    sf = sf[:, :LANES] + sf[:, LANES:2 * LANES] + sf[:, 2 * LANES:]
    lane = lax.broadcasted_iota(jnp.int32, (tm, LANES), 1)
    o_ref[...] = jnp.where(lane < 2 * heads, _sigmoid(ba), jnp.where(lane < 3 * heads, pf, sf))


def _gates(p_ba, gparams, heads):
    m = p_ba.shape[0]
    tm = min(512, m)
    return pl.pallas_call(
        functools.partial(_gates_body, heads=heads, chunk=DN_CHUNK),
        grid=(m // tm,),
        in_specs=[pl.BlockSpec((tm, LANES), lambda i: (i, 0)),
                  pl.BlockSpec((2, LANES), lambda i: (0, 0))],
        out_specs=pl.BlockSpec((tm, LANES), lambda i: (i, 0)),
        out_shape=jax.ShapeDtypeStruct((m, LANES), F32),
        compiler_params=_cparams("parallel"),
        name="gates",
    )(p_ba, gparams)


def _shortconv_body(x_ref, w_ref, o_ref, pad_scr, *, t_len, ktaps, rows, n_norm, q_blocks, q_scale):
    j = pl.program_id(1)
    zeros = jnp.zeros((PAD_ROWS, LANES), F32)
    pad_scr[0:PAD_ROWS, :] = zeros
    pad_scr[PAD_ROWS + t_len:, :] = zeros
    pad_scr[PAD_ROWS:PAD_ROWS + t_len, :] = x_ref[...]
    half = ktaps // 2
    scale = jnp.where(j < q_blocks, q_scale, 1.0)

    def chunk(c, carry):
        base = pl.multiple_of(c * rows, SUBLANES)
        win = pad_scr[pl.ds(base, rows + 2 * PAD_ROWS), :]
        acc = jnp.zeros((rows, LANES), F32)
        for k in range(ktaps):
            off = PAD_ROWS - half + k
            acc = acc + w_ref[k:k + 1, :] * win[off:off + rows, :]
        y = acc * _sigmoid(acc)
        ss = jnp.sum(y * y, axis=-1, keepdims=True)
        yn = y * (lax.rsqrt(ss + EPS) * scale)
        o_ref[pl.ds(base, rows), :] = jnp.where(j < n_norm, yn, y)
        return carry

    lax.fori_loop(0, t_len // rows, chunk, 0)


def _shortconv(p_main, conv_w, bsz, t_len, heads):
    ktaps, width = conv_w.shape
    nblk = width // LANES
    rows = min(256, t_len)
    return pl.pallas_call(
        functools.partial(_shortconv_body, t_len=t_len, ktaps=ktaps, rows=rows, n_norm=2 * heads,
                          q_blocks=heads, q_scale=float(LANES) ** -0.5),
        grid=(bsz, nblk),
        in_specs=[pl.BlockSpec((t_len, LANES), lambda b, j: (b, j)),
                  pl.BlockSpec((ktaps, LANES), lambda b, j: (0, j))],
        out_specs=pl.BlockSpec((t_len, LANES), lambda b, j: (b, j)),
        out_shape=jax.ShapeDtypeStruct((bsz * t_len, width), F32),
        scratch_shapes=[pltpu.VMEM((t_len + 2 * PAD_ROWS, LANES), F32)],
        compiler_params=_cparams("parallel", "parallel"),
        name="shortconv",
    )(p_main, conv_w)


def _delta_one(q, k, v, gcol, grow, bcol, state, rev, ii, jj, eye):
    c = q.shape[0]
    dh = q.shape[1]
    incl = (ii <= jj) if rev else (ii >= jj)
    strict = (ii < jj) if rev else (ii > jj)
    decay = jnp.exp(jnp.where(incl, gcol - grow, -jnp.inf))
    kb = k.astype(BF16)
    kk = _dot_nt(kb, kb)
    qk = _dot_nt(q.astype(BF16), kb)
    a_mat = jnp.where(strict, kk * decay * bcol, 0.0)
    pw = -a_mat
    t_mat = eye + pw
    for _ in range(int(math.log2(c)) - 1):
        pwb = pw.astype(BF16)
        pw = _dot(pwb, pwb)
        t_mat = t_mat + _dot(t_mat.astype(BF16), pw.astype(BF16))
    eg = jnp.exp(gcol)
    rhs = jnp.concatenate([v * bcol, k * (bcol * eg)], axis=1)
    uw = _dot(t_mat.astype(BF16), rhs.astype(BF16))
    u = uw[:, :dh]
    w = uw[:, dh:]
    qk = jnp.where(incl, qk * decay, 0.0)
    q_dec = q * eg
    glast = gcol[0:1, :] if rev else gcol[c - 1:c, :]
    k_dec = k * jnp.exp(glast - gcol)
    m1 = _dot(jnp.concatenate([w, q_dec], axis=0).astype(BF16), state.astype(BF16))
    v_new = u - m1[:c]
    vb = v_new.astype(BF16)
    o = m1[c:] + _dot(qk.astype(BF16), vb)
    new_state = state * jnp.exp(glast) + _dot_tn(k_dec.astype(BF16), vb)
    return o, new_state


def _delta_body(qf, kf, vf, qb, kb, vb, gf, gb, gtf, gtb, s0_ref, of_ref, ob_ref, sfin_ref, s_scr,
                *, hb, nchunks):
    n = pl.program_id(2)

    @pl.when(n == 0)
    def _():
        s_scr[...] = s0_ref[:, 0]

    c = qf.shape[0]
    ii = lax.broadcasted_iota(jnp.int32, (c, c), 0)
    jj = lax.broadcasted_iota(jnp.int32, (c, c), 1)
    eye = jnp.where(ii == jj, 1.0, 0.0)
    for d, (q_ref, k_ref, v_ref, g_ref, gt_ref, o_ref) in enumerate(
            ((qf, kf, vf, gf, gtf, of_ref), (qb, kb, vb, gb, gtb, ob_ref))):
        gblk = g_ref[0]
        gtblk = gt_ref[0, 0]
        for hh in range(hb):
            sl = slice(hh * LANES, (hh + 1) * LANES)
            cb = d * hb + hh
            cg = 2 * hb + d * hb + hh
            o, s_new = _delta_one(q_ref[:, sl], k_ref[:, sl], v_ref[:, sl],
                                  gblk[:, cg:cg + 1], gtblk[cg:cg + 1, :], gblk[:, cb:cb + 1],
                                  s_scr[d, hh], d == 1, ii, jj, eye)
            o_ref[:, sl] = o
            s_scr[d, hh] = s_new

    @pl.when(n == nchunks - 1)
    def _():
        sfin_ref[:, 0] = s_scr[...]


def _delta(qkv, gates, s0, bsz, t_len, heads, hb):
    m = qkv.shape[0]
    c = DN_CHUNK
    nch = t_len // c
    ng = heads // hb
    dh = LANES
    gg = gates[:, :4 * heads].reshape(m, 4, ng, hb).transpose(2, 0, 1, 3).reshape(ng, m, 4 * hb)
    ggt = gg.reshape(ng, m // c, c, 4 * hb).transpose(0, 1, 3, 2)
    fwd = lambda b, g, n: b * nch + n
    bwd = lambda b, g, n: b * nch + (nch - 1 - n)
    qspec = lambda row, off: pl.BlockSpec((c, hb * dh), lambda b, g, n: (row(b, g, n), off + g))
    gspec = lambda row: pl.BlockSpec((1, c, 4 * hb), lambda b, g, n: (g, row(b, g, n), 0))
    gtspec = lambda row: pl.BlockSpec((1, 1, 4 * hb, c), lambda b, g, n: (g, row(b, g, n), 0, 0))
    sspec = pl.BlockSpec((2, 1, hb, dh, dh), lambda b, g, n: (0, b, g, 0, 0))
    return pl.pallas_call(
        functools.partial(_delta_body, hb=hb, nchunks=nch),
        grid=(bsz, ng, nch),
        in_specs=[qspec(fwd, 0), qspec(fwd, ng), qspec(fwd, 2 * ng),
                  qspec(bwd, 0), qspec(bwd, ng), qspec(bwd, 2 * ng),
                  gspec(fwd), gspec(bwd), gtspec(fwd), gtspec(bwd), sspec],
        out_specs=[qspec(fwd, 0), qspec(bwd, 0), sspec],
        out_shape=[jax.ShapeDtypeStruct((m, heads * dh), F32), jax.ShapeDtypeStruct((m, heads * dh), F32),
                   jax.ShapeDtypeStruct(s0.shape, F32)],
        scratch_shapes=[pltpu.VMEM((2, hb, dh, dh), F32)],
        compiler_params=_cparams("parallel", "parallel", "arbitrary"),
        name="delta",
    )(qkv, qkv, qkv, qkv, qkv, qkv, gg, gg, ggt, ggt, s0)


def _confconv_body(a_ref, g_ref, w_ref, b_ref, o_ref, pad_scr, *, t_len, ktaps, seg, n_seg_blocks, n_blocks):
    j = pl.program_id(1)
    half = ktaps // 2
    pitch = seg + 2 * PAD_ROWS
    nseg = t_len // seg
    bias = b_ref[...]

    def glu(r0, nrows):
        a = a_ref[pl.ds(r0, nrows), :]
        return a * _sigmoid(g_ref[pl.ds(r0, nrows), :])

    if n_seg_blocks > 0:
        @pl.when(j < n_seg_blocks)
        def _():
            zeros = jnp.zeros((PAD_ROWS, LANES), F32)

            def fill(s, carry):
                base = pl.multiple_of(s * pitch, SUBLANES)
                pad_scr[pl.ds(base, PAD_ROWS), :] = zeros
                pad_scr[pl.ds(base + PAD_ROWS, seg), :] = glu(pl.multiple_of(s * seg, SUBLANES), seg)
                pad_scr[pl.ds(base + PAD_ROWS + seg, PAD_ROWS), :] = zeros
                return carry

            lax.fori_loop(0, nseg, fill, 0)
            rows = min(seg, 128)

            def conv(i, carry):
                s = i // (seg // rows)
                r = i % (seg // rows)
                base = pl.multiple_of(s * pitch + r * rows, SUBLANES)
                win = pad_scr[pl.ds(base, rows + 2 * PAD_ROWS), :]
                acc = jnp.zeros((rows, LANES), F32)
                for k in range(ktaps):
                    off = PAD_ROWS - half + k
                    acc = acc + w_ref[k:k + 1, :] * win[off:off + rows, :]
                o_ref[pl.ds(pl.multiple_of(s * seg + r * rows, SUBLANES), rows), :] = acc + bias
                return carry

            lax.fori_loop(0, nseg * (seg // rows), conv, 0)

    if n_seg_blocks < n_blocks:
        @pl.when(j >= n_seg_blocks)
        def _():
            halo = half * seg
            rows = min(t_len, 128)
            zrows = seg

            def zero(i, carry):
                z = jnp.zeros((zrows, LANES), F32)
                pad_scr[pl.ds(pl.multiple_of(i * zrows, SUBLANES), zrows), :] = z
                pad_scr[pl.ds(pl.multiple_of(halo + t_len + i * zrows, SUBLANES), zrows), :] = z
                return carry

            lax.fori_loop(0, halo // zrows, zero, 0)

            def fill(i, carry):
                r0 = pl.multiple_of(i * rows, SUBLANES)
                pad_scr[pl.ds(halo + r0, rows), :] = glu(r0, rows)
                return carry

            lax.fori_loop(0, t_len // rows, fill, 0)

            def conv(i, carry):
                r0 = pl.multiple_of(i * rows, SUBLANES)
                acc = jnp.zeros((rows, LANES), F32)
                for k in range(ktaps):
                    acc = acc + w_ref[k:k + 1, :] * pad_scr[pl.ds(r0 + k * seg, rows), :]
                o_ref[pl.ds(r0, rows), :] = acc + bias
                return carry

            lax.fori_loop(0, t_len // rows, conv, 0)


def _confconv(p_main, col0, dw, dw_b, bsz, t_len, seg, n_seg_blocks):
    ktaps, width = dw.shape
    nblk = width // LANES
    half = ktaps // 2
    assert half < PAD_ROWS and seg % SUBLANES == 0 and t_len % seg == 0
    c0 = col0 // LANES
    nseg = t_len // seg
    pad_rows = max(nseg * (seg + 2 * PAD_ROWS) if n_seg_blocks > 0 else 0,
                   t_len + 2 * half * seg if n_seg_blocks < nblk else 0)
    return pl.pallas_call(
        functools.partial(_confconv_body, t_len=t_len, ktaps=ktaps, seg=seg, n_seg_blocks=n_seg_blocks,
                          n_blocks=nblk),
        grid=(bsz, nblk),
        in_specs=[pl.BlockSpec((t_len, LANES), lambda b, j: (b, c0 + j)),
                  pl.BlockSpec((t_len, LANES), lambda b, j: (b, c0 + nblk + j)),
                  pl.BlockSpec((ktaps, LANES), lambda b, j: (0, j)),
                  pl.BlockSpec((1, LANES), lambda b, j: (0, j))],
        out_specs=pl.BlockSpec((t_len, LANES), lambda b, j: (b, j)),
        out_shape=jax.ShapeDtypeStruct((bsz * t_len, width), F32),
        scratch_shapes=[pltpu.VMEM((pad_rows, LANES), F32)],
        compiler_params=_cparams("parallel", "parallel"),
        name="confconv",
    )(p_main, p_main, dw, dw_b.reshape(1, width))


def _mixfin_body(of_ref, ob_ref, z_ref, y_ref, nw_ref, lg_ref, lb_ref, o_ref, *, heads):
    dnw = heads * LANES
    nw = nw_ref[...]
    for h in range(heads):
        sl = slice(h * LANES, (h + 1) * LANES)
        o = of_ref[:, sl] + ob_ref[:, sl]
        z = z_ref[:, sl]
        ms = jnp.mean(o * o, axis=-1, keepdims=True)
        o_ref[:, sl] = ((o * lax.rsqrt(ms + EPS)) * nw * (z * _sigmoid(z))).astype(BF16)
    y = y_ref[...]
    mu = jnp.mean(y, axis=-1, keepdims=True)
    yc = y - mu
    var = jnp.mean(yc * yc, axis=-1, keepdims=True)
    yn = yc * lax.rsqrt(var + EPS) * lg_ref[...] + lb_ref[...]
    o_ref[:, dnw:] = (yn * _sigmoid(yn)).astype(BF16)


def _mixfin(o_f, o_b, p_main, y_conf, dn_norm, ln_g, ln_b, heads):
    m, dnw = o_f.shape
    cw = y_conf.shape[1]
    tm = min(256, m)
    zblk = 3
    return pl.pallas_call(
        functools.partial(_mixfin_body, heads=heads),
        grid=(m // tm,),
        in_specs=[pl.BlockSpec((tm, dnw), lambda i: (i, 0)),
                  pl.BlockSpec((tm, dnw), lambda i: (i, 0)),
                  pl.BlockSpec((tm, dnw), lambda i: (i, zblk)),
                  pl.BlockSpec((tm, cw), lambda i: (i, 0)),
                  pl.BlockSpec((1, LANES), lambda i: (0, 0)),
                  pl.BlockSpec((1, cw), lambda i: (0, 0)),
                  pl.BlockSpec((1, cw), lambda i: (0, 0))],
        out_specs=pl.BlockSpec((tm, dnw + cw), lambda i: (i, 0)),
        out_shape=jax.ShapeDtypeStruct((m, dnw + cw), BF16),
        compiler_params=_cparams("parallel"),
        name="mixfin",
    )(o_f, o_b, p_main, y_conf, dn_norm.reshape(1, LANES), ln_g.reshape(1, cw), ln_b.reshape(1, cw))


def _outproj_body(a_ref, w_ref, x_ref, mod_ref, o_ref):
    o_ref[...] = x_ref[...] + mod_ref[0, 2:3, :] * _dot(a_ref[...], w_ref[...])


def _outproj(a, w, x2, mod, t_len):
    m, k = a.shape
    d = w.shape[1]
    per_batch = mod.shape[0] > 1
    tm = min(512, t_len if per_batch else m)
    tn = _pick(d, (1024, 512, 256, 128))
    bidx = (lambda i: (i * tm) // t_len) if per_batch else (lambda i: 0)
    return pl.pallas_call(
        _outproj_body,
        grid=(m // tm, d // tn),
        in_specs=[pl.BlockSpec((tm, k), lambda i, j: (i, 0)),
                  pl.BlockSpec((k, tn), lambda i, j: (0, j)),
                  pl.BlockSpec((tm, tn), lambda i, j: (i, j)),
                  pl.BlockSpec((1, 6, tn), lambda i, j: (bidx(i), 0, j))],
        out_specs=pl.BlockSpec((tm, tn), lambda i, j: (i, j)),
        out_shape=jax.ShapeDtypeStruct((m, d), F32),
        compiler_params=_cparams("parallel", "parallel"),
        name="outproj",
    )(a, w, x2, mod)


def _router_body(x_ref, mod_ref, wr_ref, h_ref, aff_ref):
    h = _modnorm(x_ref[...], mod_ref[0, 3:4, :], mod_ref[0, 4:5, :])
    h_ref[...] = h
    e = wr_ref.shape[0]
    h1, h2, h3 = _split3(h)
    w1, w2, w3 = _split3(wr_ref[...])
    l1 = _dot_nt(jnp.concatenate([w1, w2, w3], axis=0), h1)
    l2 = _dot_nt(jnp.concatenate([w1, w2], axis=0), h2)
    l3 = _dot_nt(w1, h3)
    logits = (l1[2 * e:] + l2[e:] + l3) + (l1[e:2 * e] + l2[:e]) + l1[:e]
    mx = jnp.max(logits, axis=0, keepdims=True)
    ex = jnp.exp(logits - mx)
    aff_ref[0] = ex / jnp.sum(ex, axis=0, keepdims=True)


def _router(x2, mod, w_router_t, bsz, t_len):
    m, d = x2.shape
    e = w_router_t.shape[0]
    per_batch = mod.shape[0] > 1
    tm = min(256, t_len)
    tpb = t_len // tm
    bidx = (lambda i: i // tpb) if per_batch else (lambda i: 0)
    return pl.pallas_call(
        _router_body,
        grid=(m // tm,),
        in_specs=[pl.BlockSpec((tm, d), lambda i: (i, 0)),
                  pl.BlockSpec((1, 6, d), lambda i: (bidx(i), 0, 0)),
                  pl.BlockSpec((e, d), lambda i: (0, 0))],
        out_specs=[pl.BlockSpec((tm, d), lambda i: (i, 0)),
                   pl.BlockSpec((1, e, tm), lambda i: (i // tpb, 0, i % tpb))],
        out_shape=[jax.ShapeDtypeStruct((m, d), F32), jax.ShapeDtypeStruct((bsz, e, t_len), F32)],
        compiler_params=_cparams("parallel"),
        name="router",
    )(x2, mod, w_router_t)


def _select_body(aff_ref, gg_ref, sl_ref, u_ref, tb_ref, o_ref, slot_scr, val_scr, *, n_exp, tb, cap):
    a = aff_ref[0]
    nrow = a.shape[0]
    bits = pltpu.bitcast(a, jnp.int32)
    gg = gg_ref[...]
    capf = float(cap)

    def group_total(mask_bf16):
        return jnp.sum(_dot(gg, mask_bf16), axis=1, keepdims=True)

    def bisect(_, carry):
        lo, hi = carry
        mid = lo + jnp.right_shift(hi - lo, 1)
        ok = group_total(jnp.where(bits >= mid, 1.0, 0.0).astype(BF16)) >= capf
        return jnp.where(ok, mid, lo), jnp.where(ok, hi, mid)

    lo0 = jnp.zeros((nrow, 1), jnp.int32)
    hi0 = jnp.full((nrow, 1), 0x7F800001, jnp.int32)
    thr, _ = lax.fori_loop(0, 32, bisect, (lo0, hi0))

    gt = bits > thr
    eq = bits == thr
    gtb = jnp.where(gt, 1.0, 0.0).astype(BF16)
    eqb = jnp.where(eq, 1.0, 0.0).astype(BF16)
    pg = _dot(gtb, u_ref[...])
    pe = _dot(eqb, u_ref[...])
    rg = jnp.broadcast_to(pg[:, LANES - 1:LANES], (nrow, LANES)).astype(BF16)
    re = jnp.broadcast_to(pe[:, LANES - 1:LANES], (nrow, LANES)).astype(BF16)
    off_g = _dot(sl_ref[...], rg)[:, 0:1]
    off_e = _dot(sl_ref[...], re)[:, 0:1]
    tot_g = _dot(gg, rg)[:, 0:1]
    need = capf - tot_g
    pos_e = off_e + pe - 1.0
    slot = jnp.where(gt, off_g + pg - 1.0, jnp.where(eq & (pos_e < need), tot_g + pos_e, -1.0))
    slot_scr[...] = slot

    lane = lax.broadcasted_iota(jnp.int32, (nrow, LANES), 1).astype(F32)
    tok = tb_ref[...] + lane
    tok_hi = jnp.floor(tok * (1.0 / 64.0))
    tok_lo = tok - 64.0 * tok_hi
    a1, a2, a3 = _split3(a)
    zero = jnp.zeros((nrow, LANES), F32)
    for c, arr in enumerate((tok_hi, tok_lo, a1.astype(F32), a2.astype(F32), a3.astype(F32), zero, zero, zero)):
        val_scr[:, c, :] = arr

    siota = lax.broadcasted_iota(jnp.int32, (cap, LANES), 0).astype(F32)
    lane8 = lax.broadcasted_iota(jnp.int32, (cap, SUBLANES), 1)
    for e in range(n_exp):
        def gather(t, acc):
            r = e * tb + t
            onehot = jnp.where(slot_scr[pl.ds(r, 1), :] == siota, 1.0, 0.0).astype(BF16)
            return acc + _dot_nt(onehot, val_scr[r].astype(BF16))

        acc = lax.fori_loop(0, tb, gather, jnp.zeros((cap, SUBLANES), F32))
        idx = acc[:, 0:1] * 64.0 + acc[:, 1:2]
        gate = acc[:, 2:3] + acc[:, 3:4] + acc[:, 4:5]
        o_ref[0, e] = jnp.where(lane8 == 0, idx, jnp.where(lane8 == 1, gate, 0.0))


def _select(aff, cap):
    bsz, n_exp, t_len = aff.shape
    tb = t_len // LANES
    nrow = n_exp * tb
    r = np.arange(nrow)
    same = (r[:, None] // tb) == (r[None, :] // tb)
    gg = jnp.asarray(same, BF16)
    sl = jnp.asarray(same & (r[None, :] < r[:, None]), BF16)
    ln = np.arange(LANES)
    u = jnp.asarray(ln[:, None] <= ln[None, :], BF16)
    tbase = jnp.asarray(((r % tb) * LANES).reshape(nrow, 1), F32)
    const = lambda shape: pl.BlockSpec(shape, lambda b: (0,) * len(shape))
    return pl.pallas_call(
        functools.partial(_select_body, n_exp=n_exp, tb=tb, cap=cap),
        grid=(bsz,),
        in_specs=[pl.BlockSpec((1, nrow, LANES), lambda b: (b, 0, 0)),
                  const((nrow, nrow)), const((nrow, nrow)), const((LANES, LANES)), const((nrow, 1))],
        out_specs=pl.BlockSpec((1, n_exp, cap, SUBLANES), lambda b: (b, 0, 0, 0)),
        out_shape=jax.ShapeDtypeStruct((bsz, n_exp, cap, SUBLANES), F32),
        scratch_shapes=[pltpu.VMEM((nrow, LANES), F32), pltpu.VMEM((nrow, SUBLANES, LANES), F32)],
        compiler_params=_cparams("parallel"),
        name="select",
    )(aff.reshape(bsz, nrow, LANES), gg, sl, u, tbase)


def _expert_body(idx_ref, g_ref, mod_ref, wg_ref, wu_ref, wd_ref, h_hbm, x_hbm, o_hbm,
                 hs_scr, xs_scr, sem_h, sem_x, sem_o, *, t_len, cap, rows):
    b = pl.program_id(1)
    base = b * t_len
    wg = wg_ref[0].astype(BF16)
    wu = wu_ref[0].astype(BF16)
    wd = wd_ref[0].astype(BF16)
    gate_mod = mod_ref[0, 5:6, :]
    for part in range(cap // rows):
        s0 = part * rows

        def start_gather(s, carry):
            tok = base + idx_ref[0, 0, s0 + s]
            pltpu.make_async_copy(h_hbm.at[pl.ds(tok, 1), :], hs_scr.at[pl.ds(s, 1), :], sem_h).start()
            pltpu.make_async_copy(x_hbm.at[pl.ds(tok, 1), :], xs_scr.at[pl.ds(s, 1), :], sem_x).start()
            return carry

        lax.fori_loop(0, rows, start_gather, 0)
        pltpu.make_async_copy(h_hbm.at[pl.ds(0, rows), :], hs_scr, sem_h).wait()
        pltpu.make_async_copy(x_hbm.at[pl.ds(0, rows), :], xs_scr, sem_x).wait()

        hb = hs_scr[...].astype(BF16)
        a = _dot(hb, wg)
        up = _dot(hb, wu)
        act = (a * _sigmoid(a)) * up
        y = _dot(act.astype(BF16), wd) * g_ref[0, s0:s0 + rows, :]
        xs_scr[...] = xs_scr[...] + gate_mod * y

        def start_scatter(s, carry):
            tok = base + idx_ref[0, 0, s0 + s]
            pltpu.make_async_copy(xs_scr.at[pl.ds(s, 1), :], o_hbm.at[pl.ds(tok, 1), :], sem_o).start()
            return carry

        lax.fori_loop(0, rows, start_scatter, 0)
        pltpu.make_async_copy(xs_scr, o_hbm.at[pl.ds(0, rows), :], sem_o).wait()


def _experts(h2, x2, idx, gsel, mod, w_gate, w_up, w_down, bsz, t_len):
    m, d = x2.shape
    n_exp, cap = idx.shape[1], idx.shape[2]
    ff = w_gate.shape[-1]
    rows = min(cap, 256)
    nb = mod.shape[0]
    assert bsz >= 2, "consecutive grid steps must touch different samples"
    idx_eb = idx.transpose(1, 0, 2).reshape(n_exp * bsz, 1, cap)
    g_eb = gsel.transpose(1, 0, 2).reshape(n_exp * bsz, cap, 1)
    any_spec = pl.BlockSpec(memory_space=pl.ANY)
    return pl.pallas_call(
        functools.partial(_expert_body, t_len=t_len, cap=cap, rows=rows),
        grid=(n_exp, bsz),
        in_specs=[pl.BlockSpec((1, 1, cap), lambda e, b: (e * bsz + b, 0, 0), memory_space=pltpu.SMEM),
                  pl.BlockSpec((1, cap, 1), lambda e, b: (e * bsz + b, 0, 0)),
                  pl.BlockSpec((1, 6, d), lambda e, b: (b if nb > 1 else 0, 0, 0)),
                  pl.BlockSpec((1, d, ff), lambda e, b: (e, 0, 0)),
                  pl.BlockSpec((1, d, ff), lambda e, b: (e, 0, 0)),
                  pl.BlockSpec((1, ff, d), lambda e, b: (e, 0, 0)),
                  any_spec, any_spec],
        out_specs=any_spec,
        out_shape=jax.ShapeDtypeStruct((m, d), F32),
        scratch_shapes=[pltpu.VMEM((rows, d), F32), pltpu.VMEM((rows, d), F32),
                        pltpu.SemaphoreType.DMA(()), pltpu.SemaphoreType.DMA(()), pltpu.SemaphoreType.DMA(())],
        input_output_aliases={7: 0},
        compiler_params=_cparams("arbitrary", "arbitrary"),
        name="experts",
    )(idx_eb, g_eb, mod, w_gate, w_up, w_down, h2, x2)


def _final_body(x_ref, w_ref, o_ref):
    x = x_ref[...]
    ms = jnp.mean(x * x, axis=-1, keepdims=True)
    o_ref[...] = (x * lax.rsqrt(ms + EPS)) * w_ref[...]


def _final_norm(x2, w):
    m, d = x2.shape
    tm = min(512, m)
    return pl.pallas_call(
        _final_body,
        grid=(m // tm,),
        in_specs=[pl.BlockSpec((tm, d), lambda i: (i, 0)), pl.BlockSpec((1, d), lambda i: (0, 0))],
        out_specs=pl.BlockSpec((tm, d), lambda i: (i, 0)),
        out_shape=jax.ShapeDtypeStruct((m, d), F32),
        compiler_params=_cparams("parallel"),
        name="final_norm",
    )(x2, w.reshape(1, d))


def _moe(x2, mod, lw, bsz, t_len):
    n_exp = lw["w_router_t"].shape[0]
    cap = (CAP_FACTOR * t_len) // n_exp
    h2, aff = _router(x2, mod, lw["w_router_t"], bsz, t_len)
    sel = _select(aff, cap)
    idx = sel[..., 0].astype(jnp.int32)
    return _experts(h2, x2, idx, sel[..., 1], mod, lw["w_gate"], lw["w_up"], lw["w_down"], bsz, t_len)


def _mixer_front(x2, mod, lw, s0, bsz, t_len, heads, hb):
    p_main, p_ba = _inproj(x2, mod, lw["w_main"], lw["w_ba"], t_len)
    gates = _gates(p_ba, lw["gparams"], heads)
    qkv = _shortconv(p_main, lw["conv_qkv"], bsz, t_len, heads)
    o_f, o_b, s_fin = _delta(qkv, gates, s0, bsz, t_len, heads, hb)
    return p_main, o_f, o_b, s_fin


def _mixer_back(x2, mod, lw, p_main, o_f, o_b, bsz, t_len, heads, on_grid):
    cw = lw["conf_dw"].shape[1]
    nblk = cw // LANES
    if on_grid:
        y = _confconv(p_main, 4 * heads * LANES, lw["conf_dw"], lw["conf_dw_b"], bsz, t_len, GRID_W, nblk // 2)
    else:
        y = _confconv(p_main, 4 * heads * LANES, lw["conf_dw"], lw["conf_dw_b"], bsz, t_len, t_len, nblk)
    cat = _mixfin(o_f, o_b, p_main, y, lw["dn_norm"], lw["conf_ln_g"], lw["conf_ln_b"], heads)
    return _outproj(cat, lw["w_out"], x2, mod, t_len)


def kernel(x, c, ctx, c_ctx, w_ada, b_ada, w_in, conv_qkv, a_log, dt_bias, dn_norm, conf_dw, conf_dw_b,
           conf_ln_g, conf_ln_b, w_out, w_router, w_gate_e, w_up_e, w_down_e, final_norm):
    bsz, seq, d = x.shape
    ctx_len = ctx.shape[1]
    depth = w_ada.shape[0]
    dh = dn_norm.shape[-1]
    dnw = conv_qkv.shape[-1] // 3
    heads = dnw // dh
    cw = conf_dw.shape[-1]
    conf_off = w_in.shape[-1] - 2 * cw
    assert dh == LANES and 4 * heads <= LANES and conf_off == 4 * dnw + 4 * heads
    hb = min(4, heads)

    nb = -(-(bsz + 1) // SUBLANES) * SUBLANES
    cond = jnp.zeros((nb, d), F32).at[:bsz].set(c).at[bsz].set(c_ctx)
    mods = _adaln(cond, w_ada, b_ada).reshape(depth, nb, 6, d)

    w_main = jnp.concatenate([w_in[:, :, :4 * dnw], w_in[:, :, conf_off:]], axis=-1).astype(BF16)
    w_ba = jnp.pad(w_in[:, :, 4 * dnw:conf_off], ((0, 0), (0, 0), (0, LANES - 4 * heads))).astype(BF16)
    w_out_b = w_out.astype(BF16)
    w_router_t = jnp.swapaxes(w_router, 1, 2)
    gparams = jnp.zeros((depth, 2, LANES), F32)
    gparams = gparams.at[:, 0, 2 * heads:4 * heads].set(a_log.reshape(depth, 2 * heads))
    gparams = gparams.at[:, 1, 2 * heads:4 * heads].set(dt_bias.reshape(depth, 2 * heads))

    x_lat = x.reshape(bsz * seq, d)
    x_ctx = ctx.reshape(bsz * ctx_len, d)
    zero_state = jnp.zeros((2, bsz, heads, dh, dh), F32)
    for l in range(depth):
        last = l == depth - 1
        lw = dict(w_main=w_main[l], w_ba=w_ba[l], gparams=gparams[l], conv_qkv=conv_qkv[l], dn_norm=dn_norm[l],
                  conf_dw=conf_dw[l], conf_dw_b=conf_dw_b[l], conf_ln_g=conf_ln_g[l], conf_ln_b=conf_ln_b[l],
                  w_out=w_out_b[l], w_router_t=w_router_t[l], w_gate=w_gate_e[l], w_up=w_up_e[l],
                  w_down=w_down_e[l])
        ml = mods[l, :bsz]
        mc = mods[l, bsz:bsz + 1]
        pc, ofc, obc, ctx_states = _mixer_front(x_ctx, mc, lw, zero_state, bsz, ctx_len, heads, hb)
        pl_, ofl, obl, _ = _mixer_front(x_lat, ml, lw, ctx_states, bsz, seq, heads, hb)
        x_lat = _mixer_back(x_lat, ml, lw, pl_, ofl, obl, bsz, seq, heads, True)
        x_lat = _moe(x_lat, ml, lw, bsz, seq)
        if not last:
            x_ctx = _mixer_back(x_ctx, mc, lw, pc, ofc, obc, bsz, ctx_len, heads, False)
            x_ctx = _moe(x_ctx, mc, lw, bsz, ctx_len)
    return _final_norm(x_lat, final_norm).reshape(bsz, seq, d)
```

```python
import functools
import math

import numpy as np
import jax
import jax.numpy as jnp
from jax import lax
from jax.experimental import pallas as pl
from jax.experimental.pallas import tpu as pltpu

F32 = jnp.float32
BF16 = jnp.bfloat16

EPS = 1e-6
GRID_W = 64
CAP_FACTOR = 2
DN_CHUNK = 64
LANES = 128
SUBLANES = 8
PAD_ROWS = 16
VMEM_LIMIT_BYTES = 60 * 1024 * 1024


def _cparams(*sem):
    return pltpu.CompilerParams(dimension_semantics=sem, vmem_limit_bytes=VMEM_LIMIT_BYTES)


def _pick(n, cands):
    for c in cands:
        if n % c == 0:
            return c
    return n


def _sigmoid(x):
    return 1.0 / (1.0 + jnp.exp(-x))


def _dot(a, b):
    return jnp.dot(a, b, preferred_element_type=F32)


def _dot_nt(a, b):
    return lax.dot_general(a, b, (((1,), (1,)), ((), ())), preferred_element_type=F32)


def _dot_tn(a, b):
    return lax.dot_general(a, b, (((0,), (0,)), ((), ())), preferred_element_type=F32)


def _split3(x):
    x1 = x.astype(BF16)
    r1 = x - x1.astype(F32)
    x2 = r1.astype(BF16)
    r2 = r1 - x2.astype(F32)
    return x1, x2, r2.astype(BF16)


def _ada_body(c_ref, w_ref, b_ref, o_ref):
    c = c_ref[...]
    s = (c * _sigmoid(c)).astype(BF16)
    o_ref[0] = _dot(s, w_ref[0].astype(BF16)) + b_ref[0]


def _adaln(cond, w_ada, b_ada):
    depth, d, n6 = w_ada.shape
    nb = cond.shape[0]
    tn = _pick(n6, (512, 256, 128))
    return pl.pallas_call(
        _ada_body,
        grid=(depth, n6 // tn),
        in_specs=[pl.BlockSpec((nb, d), lambda l, j: (0, 0)),
                  pl.BlockSpec((1, d, tn), lambda l, j: (l, 0, j)),
                  pl.BlockSpec((1, 1, tn), lambda l, j: (l, 0, j))],
        out_specs=pl.BlockSpec((1, nb, tn), lambda l, j: (l, 0, j)),
        out_shape=jax.ShapeDtypeStruct((depth, nb, n6), F32),
        compiler_params=_cparams("parallel", "parallel"),
        name="adaln",
    )(cond, w_ada, b_ada.reshape(depth, 1, n6))


def _modnorm(x, shift, scale):
    ms = jnp.mean(x * x, axis=-1, keepdims=True)
    return (x * lax.rsqrt(ms + EPS)) * (1.0 + scale) + shift


def _inproj_body(x_ref, mod_ref, w_ref, wba_ref, o_ref, oba_ref, h_scr):
    @pl.when(pl.program_id(1) == 0)
    def _():
        rows = _pick(x_ref.shape[0], (256, 128, 64))

        def chunk(i, carry):
            sl = pl.ds(pl.multiple_of(i * rows, rows), rows)
            h = _modnorm(x_ref[sl, :], mod_ref[0, 0:1, :], mod_ref[0, 1:2, :]).astype(BF16)
            h_scr[sl, :] = h
            oba_ref[sl, :] = _dot(h, wba_ref[...])
            return carry

        lax.fori_loop(0, x_ref.shape[0] // rows, chunk, 0)

    o_ref[...] = _dot(h_scr[...], w_ref[...])


def _inproj(x2, mod, w_main, w_ba, layer, t_len):
    m, d = x2.shape
    n = w_main.shape[2]
    per_batch = mod.shape[0] > 1
    tm = _pick(t_len if per_batch else m, (1024, 512, 256, 128, 64))
    tn = _pick(n, (512, 256, 128))
    bidx = (lambda i: (i * tm) // t_len) if per_batch else (lambda i: 0)
    return pl.pallas_call(
        _inproj_body,
        grid=(m // tm, n // tn),
        in_specs=[pl.BlockSpec((tm, d), lambda i, j: (i, 0), pipeline_mode=pl.Buffered(1)),
                  pl.BlockSpec((1, 6, d), lambda i, j: (bidx(i), 0, 0)),
                  pl.BlockSpec((None, d, tn), lambda i, j: (layer, 0, j)),
                  pl.BlockSpec((None, d, LANES), lambda i, j: (layer, 0, 0))],
        out_specs=[pl.BlockSpec((tm, tn), lambda i, j: (i, j)),
                   pl.BlockSpec((tm, LANES), lambda i, j: (i, 0))],
        out_shape=[jax.ShapeDtypeStruct((m, n), F32), jax.ShapeDtypeStruct((m, LANES), F32)],
        scratch_shapes=[pltpu.VMEM((tm, d), BF16)],
        compiler_params=_cparams("parallel", "arbitrary"),
        name="inproj",
    )(x2, mod, w_main, w_ba)


def _gates_body(ba_ref, gp_ref, o_ref, *, heads, chunk):
    tm = ba_ref.shape[0]
    ba = ba_ref[...]
    xs = ba + gp_ref[1:2, :]
    softplus = jnp.maximum(xs, 0.0) + jnp.log1p(jnp.exp(-jnp.abs(xs)))
    g = -jnp.exp(gp_ref[0:1, :]) * softplus
    g1, g2, g3 = _split3(g)
    gs = jnp.concatenate([g1, g2, g3], axis=1)
    ii = lax.broadcasted_iota(jnp.int32, (tm, tm), 0)
    jj = lax.broadcasted_iota(jnp.int32, (tm, tm), 1)
    sh = int(math.log2(chunk))
    same = jnp.right_shift(ii, sh) == jnp.right_shift(jj, sh)
    lo = jnp.where(same & (jj <= ii), 1.0, 0.0).astype(BF16)
    up = jnp.where(same & (jj >= ii), 1.0, 0.0).astype(BF16)
    pf = _dot(lo, gs)
    sf = _dot(up, gs)
    pf = pf[:, :LANES] + pf[:, LANES:2 * LANES] + pf[:, 2 * LANES:]
    sf = sf[:, :LANES] + sf[:, LANES:2 * LANES] + sf[:, 2 * LANES:]
    lane = lax.broadcasted_iota(jnp.int32, (tm, LANES), 1)
    o_ref[...] = jnp.where(lane < 2 * heads, _sigmoid(ba), jnp.where(lane < 3 * heads, pf, sf))


def _gates(p_ba, gparams, heads):
    m = p_ba.shape[0]
    tm = _pick(m, (512, 256, 128, DN_CHUNK))
    return pl.pallas_call(
        functools.partial(_gates_body, heads=heads, chunk=DN_CHUNK),
        grid=(m // tm,),
        in_specs=[pl.BlockSpec((tm, LANES), lambda i: (i, 0)),
                  pl.BlockSpec((2, LANES), lambda i: (0, 0))],
        out_specs=pl.BlockSpec((tm, LANES), lambda i: (i, 0)),
        out_shape=jax.ShapeDtypeStruct((m, LANES), F32),
        compiler_params=_cparams("parallel"),
        name="gates",
    )(p_ba, gparams)


def _shortconv_body(x_ref, w_ref, o_ref, pad_scr, *, t_len, ktaps, rows, n_norm, q_blocks, q_scale):
    j = pl.program_id(1)
    zeros = jnp.zeros((PAD_ROWS, LANES), F32)
    pad_scr[0:PAD_ROWS, :] = zeros
    pad_scr[PAD_ROWS + t_len:, :] = zeros
    pad_scr[PAD_ROWS:PAD_ROWS + t_len, :] = x_ref[...]
    half = ktaps // 2
    scale = jnp.where(j < q_blocks, q_scale, 1.0)

    def chunk(c, carry):
        base = pl.multiple_of(c * rows, SUBLANES)
        acc = jnp.zeros((rows, LANES), F32)
        for k in range(ktaps):
            off = PAD_ROWS - half + k
            acc = acc + w_ref[k:k + 1, :] * pad_scr[pl.ds(base + off, rows), :]
        y = acc * _sigmoid(acc)
        ss = jnp.sum(y * y, axis=-1, keepdims=True)
        yn = y * (lax.rsqrt(ss + EPS) * scale)
        o_ref[pl.ds(base, rows), :] = jnp.where(j < n_norm, yn, y)
        return carry

    lax.fori_loop(0, t_len // rows, chunk, 0, unroll=2 if (t_len // rows) % 2 == 0 else 1)


def _shortconv(p_main, conv_w, bsz, t_len, heads):
    ktaps, width = conv_w.shape
    nblk = width // LANES
    rows = min(128, t_len)
    return pl.pallas_call(
        functools.partial(_shortconv_body, t_len=t_len, ktaps=ktaps, rows=rows, n_norm=2 * heads,
                          q_blocks=heads, q_scale=float(LANES) ** -0.5),
        grid=(bsz, nblk),
        in_specs=[pl.BlockSpec((t_len, LANES), lambda b, j: (b, j)),
                  pl.BlockSpec((ktaps, LANES), lambda b, j: (0, j))],
        out_specs=pl.BlockSpec((t_len, LANES), lambda b, j: (b, j)),
        out_shape=jax.ShapeDtypeStruct((bsz * t_len, width), F32),
        scratch_shapes=[pltpu.VMEM((t_len + 2 * PAD_ROWS, LANES), F32)],
        compiler_params=_cparams("parallel", "parallel"),
        name="shortconv",
    )(p_main, conv_w)


def _delta_body(qf, kf, vf, qb, kb, vb, gf, gb, gtf, gtb, s0_ref, of_ref, ob_ref, sfin_ref, s_scr,
                *, hb, nchunks):
    n = pl.program_id(2)

    @pl.when(n == 0)
    def _():
        s_scr[...] = s0_ref[:, 0]

    c = qf.shape[0]
    dh = LANES
    nfac = int(math.log2(c))
    ii = lax.broadcasted_iota(jnp.int32, (c, c), 0)
    jj = lax.broadcasted_iota(jnp.int32, (c, c), 1)
    dirs = ((qf, kf, vf, gf, gtf, of_ref, ii >= jj, ii > jj), (qb, kb, vb, gb, gtb, ob_ref, ii <= jj, ii < jj))
    inst = [(d, hh) for d in range(2) for hh in range(hb)]

    def per_inst(fn):
        return [fn(i, d, hh) for i, (d, hh) in enumerate(inst)]

    def load(i, d, hh):
        q_ref, k_ref, v_ref, g_ref, gt_ref = dirs[d][:5]
        sl = slice(hh * LANES, (hh + 1) * LANES)
        cb = d * hb + hh
        cg = 2 * hb + d * hb + hh
        gcol = g_ref[0, :, cg:cg + 1]
        return dict(q=q_ref[:, sl], k=k_ref[:, sl], v=v_ref[:, sl], gcol=gcol, grow=gt_ref[0, 0, cg:cg + 1, :],
                    bcol=g_ref[0, :, cb:cb + 1], glast=gcol[0:1, :] if d == 1 else gcol[c - 1:c, :])

    t = per_inst(load)
    kq = per_inst(lambda i, d, hh: _dot_nt(jnp.concatenate([t[i]["k"], t[i]["q"]], axis=0).astype(BF16),
                                           t[i]["k"].astype(BF16)))
    decay = per_inst(lambda i, d, hh: jnp.exp(jnp.where(dirs[d][6], t[i]["gcol"] - t[i]["grow"], -jnp.inf)))
    eg = per_inst(lambda i, d, hh: jnp.exp(t[i]["gcol"]))
    eye = jnp.where(ii == jj, 1.0, 0.0)
    pw = per_inst(lambda i, d, hh: jnp.where(dirs[d][7], -(kq[i][:c] * decay[i] * t[i]["bcol"]), 0.0))
    rhs = per_inst(lambda i, d, hh: jnp.concatenate([t[i]["v"] * t[i]["bcol"],
                                                     t[i]["k"] * (t[i]["bcol"] * eg[i])], axis=1).astype(BF16))
    tm = per_inst(lambda i, d, hh: eye + pw[i])
    pwb = per_inst(lambda i, d, hh: pw[i].astype(BF16))
    pwb = per_inst(lambda i, d, hh: _dot(pwb[i], pwb[i]).astype(BF16))
    for j in range(1, nfac):
        tp = per_inst(lambda i, d, hh: _dot(tm[i].astype(BF16), pwb[i]))
        if j < nfac - 1:
            pwb = per_inst(lambda i, d, hh: _dot(pwb[i], pwb[i]).astype(BF16))
        tm = per_inst(lambda i, d, hh: tm[i] + tp[i])
    x = per_inst(lambda i, d, hh: _dot(tm[i].astype(BF16), rhs[i]))
    state = per_inst(lambda i, d, hh: s_scr[d, hh])
    m1 = per_inst(lambda i, d, hh: _dot(jnp.concatenate([x[i][:, dh:], t[i]["q"] * eg[i]], axis=0).astype(BF16),
                                        state[i].astype(BF16)))
    vnew = per_inst(lambda i, d, hh: (x[i][:, :dh] - m1[i][:c]).astype(BF16))
    qkm = per_inst(lambda i, d, hh: jnp.where(dirs[d][6], kq[i][c:] * decay[i], 0.0).astype(BF16))
    kdec = per_inst(lambda i, d, hh: (t[i]["k"] * jnp.exp(t[i]["glast"] - t[i]["gcol"])).astype(BF16))
    oi = per_inst(lambda i, d, hh: _dot(qkm[i], vnew[i]))
    ds = per_inst(lambda i, d, hh: _dot_tn(kdec[i], vnew[i]))
    for i, (d, hh) in enumerate(inst):
        sl = slice(hh * LANES, (hh + 1) * LANES)
        dirs[d][5][:, sl] = m1[i][c:] + oi[i]
        s_scr[d, hh] = state[i] * jnp.exp(t[i]["glast"]) + ds[i]

    @pl.when(n == nchunks - 1)
    def _():
        sfin_ref[:, 0] = s_scr[...]


def _delta(qkv, gates, s0, bsz, t_len, heads, hb):
    m = qkv.shape[0]
    c = DN_CHUNK
    nch = t_len // c
    ng = heads // hb
    dh = LANES
    gg = gates[:, :4 * heads].reshape(m, 4, ng, hb).transpose(2, 0, 1, 3).reshape(ng, m, 4 * hb)
    ggt = gg.reshape(ng, m // c, c, 4 * hb).transpose(0, 1, 3, 2)
    fwd = lambda b, g, n: b * nch + n
    bwd = lambda b, g, n: b * nch + (nch - 1 - n)
    qspec = lambda row, off: pl.BlockSpec((c, hb * dh), lambda b, g, n: (row(b, g, n), off + g))
    gspec = lambda row: pl.BlockSpec((1, c, 4 * hb), lambda b, g, n: (g, row(b, g, n), 0))
    gtspec = lambda row: pl.BlockSpec((1, 1, 4 * hb, c), lambda b, g, n: (g, row(b, g, n), 0, 0))
    sspec = pl.BlockSpec((2, 1, hb, dh, dh), lambda b, g, n: (0, b, g, 0, 0))
    return pl.pallas_call(
        functools.partial(_delta_body, hb=hb, nchunks=nch),
        grid=(bsz, ng, nch),
        in_specs=[qspec(fwd, 0), qspec(fwd, ng), qspec(fwd, 2 * ng),
                  qspec(bwd, 0), qspec(bwd, ng), qspec(bwd, 2 * ng),
                  gspec(fwd), gspec(bwd), gtspec(fwd), gtspec(bwd), sspec],
        out_specs=[qspec(fwd, 0), qspec(bwd, 0), sspec],
        out_shape=[jax.ShapeDtypeStruct((m, heads * dh), F32), jax.ShapeDtypeStruct((m, heads * dh), F32),
                   jax.ShapeDtypeStruct(s0.shape, F32)],
        scratch_shapes=[pltpu.VMEM((2, hb, dh, dh), F32)],
        compiler_params=_cparams("parallel", "parallel", "arbitrary"),
        name="delta",
    )(qkv, qkv, qkv, qkv, qkv, qkv, gg, gg, ggt, ggt, s0)


def _confconv_body(a_ref, g_ref, w_ref, b_ref, o_ref, pad_scr, *, t_len, ktaps, seg, n_seg_blocks, n_blocks):
    j = pl.program_id(1)
    half = ktaps // 2
    pitch = seg + 2 * PAD_ROWS
    nseg = t_len // seg
    bias = b_ref[...]

    def glu(r0, nrows):
        a = a_ref[pl.ds(r0, nrows), :]
        return a * _sigmoid(g_ref[pl.ds(r0, nrows), :])

    if n_seg_blocks > 0:
        @pl.when(j < n_seg_blocks)
        def _():
            zeros = jnp.zeros((PAD_ROWS, LANES), F32)

            def fill(s, carry):
                base = pl.multiple_of(s * pitch, SUBLANES)
                pad_scr[pl.ds(base, PAD_ROWS), :] = zeros
                pad_scr[pl.ds(base + PAD_ROWS, seg), :] = glu(pl.multiple_of(s * seg, SUBLANES), seg)
                pad_scr[pl.ds(base + PAD_ROWS + seg, PAD_ROWS), :] = zeros
                return carry

            lax.fori_loop(0, nseg, fill, 0)
            rows = min(seg, 128)

            def conv(i, carry):
                s = i // (seg // rows)
                r = i % (seg // rows)
                base = pl.multiple_of(s * pitch + r * rows, SUBLANES)
                win = pad_scr[pl.ds(base, rows + 2 * PAD_ROWS), :]
                acc = jnp.zeros((rows, LANES), F32)
                for k in range(ktaps):
                    off = PAD_ROWS - half + k
                    acc = acc + w_ref[k:k + 1, :] * win[off:off + rows, :]
                o_ref[pl.ds(pl.multiple_of(s * seg + r * rows, SUBLANES), rows), :] = acc + bias
                return carry

            lax.fori_loop(0, nseg * (seg // rows), conv, 0)

    if n_seg_blocks < n_blocks:
        @pl.when(j >= n_seg_blocks)
        def _():
            halo = half * seg
            rows = min(t_len, 128)
            zrows = seg

            def zero(i, carry):
                z = jnp.zeros((zrows, LANES), F32)
                pad_scr[pl.ds(pl.multiple_of(i * zrows, SUBLANES), zrows), :] = z
                pad_scr[pl.ds(pl.multiple_of(halo + t_len + i * zrows, SUBLANES), zrows), :] = z
                return carry

            lax.fori_loop(0, halo // zrows, zero, 0)

            def fill(i, carry):
                r0 = pl.multiple_of(i * rows, SUBLANES)
                pad_scr[pl.ds(halo + r0, rows), :] = glu(r0, rows)
                return carry

            lax.fori_loop(0, t_len // rows, fill, 0)

            def conv(i, carry):
                r0 = pl.multiple_of(i * rows, SUBLANES)
                acc = jnp.zeros((rows, LANES), F32)
                for k in range(ktaps):
                    acc = acc + w_ref[k:k + 1, :] * pad_scr[pl.ds(r0 + k * seg, rows), :]
                o_ref[pl.ds(r0, rows), :] = acc + bias
                return carry

            lax.fori_loop(0, t_len // rows, conv, 0)


def _confconv(p_main, col0, dw, dw_b, bsz, t_len, seg, n_seg_blocks):
    ktaps, width = dw.shape
    nblk = width // LANES
    half = ktaps // 2
    assert half < PAD_ROWS and seg % SUBLANES == 0 and t_len % seg == 0
    c0 = col0 // LANES
    nseg = t_len // seg
    pad_rows = max(nseg * (seg + 2 * PAD_ROWS) if n_seg_blocks > 0 else 0,
                   t_len + 2 * half * seg if n_seg_blocks < nblk else 0)
    return pl.pallas_call(
        functools.partial(_confconv_body, t_len=t_len, ktaps=ktaps, seg=seg, n_seg_blocks=n_seg_blocks,
                          n_blocks=nblk),
        grid=(bsz, nblk),
        in_specs=[pl.BlockSpec((t_len, LANES), lambda b, j: (b, c0 + j)),
                  pl.BlockSpec((t_len, LANES), lambda b, j: (b, c0 + nblk + j)),
                  pl.BlockSpec((ktaps, LANES), lambda b, j: (0, j)),
                  pl.BlockSpec((1, LANES), lambda b, j: (0, j))],
        out_specs=pl.BlockSpec((t_len, LANES), lambda b, j: (b, j)),
        out_shape=jax.ShapeDtypeStruct((bsz * t_len, width), F32),
        scratch_shapes=[pltpu.VMEM((pad_rows, LANES), F32)],
        compiler_params=_cparams("parallel", "parallel"),
        name="confconv",
    )(p_main, p_main, dw, dw_b.reshape(1, width))


def _mixfin_body(of_ref, ob_ref, z_ref, y_ref, nw_ref, lg_ref, lb_ref, o_ref, *, heads):
    dnw = heads * LANES
    nw = nw_ref[...]
    for h in range(heads):
        sl = slice(h * LANES, (h + 1) * LANES)
        o = of_ref[:, sl] + ob_ref[:, sl]
        z = z_ref[:, sl]
        ms = jnp.mean(o * o, axis=-1, keepdims=True)
        o_ref[:, sl] = ((o * lax.rsqrt(ms + EPS)) * nw * (z * _sigmoid(z))).astype(BF16)
    y = y_ref[...]
    mu = jnp.mean(y, axis=-1, keepdims=True)
    yc = y - mu
    var = jnp.mean(yc * yc, axis=-1, keepdims=True)
    yn = yc * lax.rsqrt(var + EPS) * lg_ref[...] + lb_ref[...]
    o_ref[:, dnw:] = (yn * _sigmoid(yn)).astype(BF16)


def _mixfin(o_f, o_b, p_main, y_conf, dn_norm, ln_g, ln_b, heads):
    m, dnw = o_f.shape
    cw = y_conf.shape[1]
    tm = _pick(m, (256, 128, 64))
    zblk = 3
    return pl.pallas_call(
        functools.partial(_mixfin_body, heads=heads),
        grid=(m // tm,),
        in_specs=[pl.BlockSpec((tm, dnw), lambda i: (i, 0)),
                  pl.BlockSpec((tm, dnw), lambda i: (i, 0)),
                  pl.BlockSpec((tm, dnw), lambda i: (i, zblk)),
                  pl.BlockSpec((tm, cw), lambda i: (i, 0)),
                  pl.BlockSpec((1, LANES), lambda i: (0, 0)),
                  pl.BlockSpec((1, cw), lambda i: (0, 0)),
                  pl.BlockSpec((1, cw), lambda i: (0, 0))],
        out_specs=pl.BlockSpec((tm, dnw + cw), lambda i: (i, 0)),
        out_shape=jax.ShapeDtypeStruct((m, dnw + cw), BF16),
        compiler_params=_cparams("parallel"),
        name="mixfin",
    )(o_f, o_b, p_main, y_conf, dn_norm.reshape(1, LANES), ln_g.reshape(1, cw), ln_b.reshape(1, cw))


def _outproj_body(a_ref, w_ref, x_ref, mod_ref, o_ref):
    o_ref[...] = x_ref[...] + mod_ref[0, 2:3, :] * _dot(a_ref[...], w_ref[...])


def _outproj(a, w, layer, x2, mod, t_len):
    m, k = a.shape
    d = w.shape[2]
    per_batch = mod.shape[0] > 1
    tm = _pick(t_len if per_batch else m, (1024, 512, 256, 128, 64))
    tn = _pick(d, (512, 256, 128))
    bidx = (lambda i: (i * tm) // t_len) if per_batch else (lambda i: 0)
    return pl.pallas_call(
        _outproj_body,
        grid=(m // tm, d // tn),
        in_specs=[pl.BlockSpec((tm, k), lambda i, j: (i, 0)),
                  pl.BlockSpec((None, k, tn), lambda i, j: (layer, 0, j)),
                  pl.BlockSpec((tm, tn), lambda i, j: (i, j)),
                  pl.BlockSpec((1, 6, tn), lambda i, j: (bidx(i), 0, j))],
        out_specs=pl.BlockSpec((tm, tn), lambda i, j: (i, j)),
        out_shape=jax.ShapeDtypeStruct((m, d), F32),
        compiler_params=_cparams("parallel", "parallel"),
        name="outproj",
    )(a, w, x2, mod)


def _router_body(x_ref, mod_ref, wr_ref, h_ref, aff_ref):
    h = _modnorm(x_ref[...], mod_ref[0, 3:4, :], mod_ref[0, 4:5, :])
    h_ref[...] = h
    e = wr_ref.shape[0]
    h1, h2, h3 = _split3(h)
    w1, w2, w3 = _split3(wr_ref[...])
    l1 = _dot_nt(jnp.concatenate([w1, w2, w3], axis=0), h1)
    l2 = _dot_nt(jnp.concatenate([w1, w2], axis=0), h2)
    l3 = _dot_nt(w1, h3)
    logits = (l1[2 * e:] + l2[e:] + l3) + (l1[e:2 * e] + l2[:e]) + l1[:e]
    mx = jnp.max(logits, axis=0, keepdims=True)
    ex = jnp.exp(logits - mx)
    aff_ref[0] = ex / jnp.sum(ex, axis=0, keepdims=True)


def _router(x2, mod, w_router_t, bsz, t_len):
    m, d = x2.shape
    e = w_router_t.shape[0]
    per_batch = mod.shape[0] > 1
    tm = _pick(t_len, (256, 128))
    tpb = t_len // tm
    bidx = (lambda i: i // tpb) if per_batch else (lambda i: 0)
    return pl.pallas_call(
        _router_body,
        grid=(m // tm,),
        in_specs=[pl.BlockSpec((tm, d), lambda i: (i, 0)),
                  pl.BlockSpec((1, 6, d), lambda i: (bidx(i), 0, 0)),
                  pl.BlockSpec((e, d), lambda i: (0, 0))],
        out_specs=[pl.BlockSpec((tm, d), lambda i: (i, 0)),
                   pl.BlockSpec((1, e, tm), lambda i: (i // tpb, 0, i % tpb))],
        out_shape=[jax.ShapeDtypeStruct((m, d), F32), jax.ShapeDtypeStruct((bsz, e, t_len), F32)],
        compiler_params=_cparams("parallel"),
        name="router",
    )(x2, mod, w_router_t)


def _select_body(aff_ref, gg_ref, sl_ref, u_ref, tb_ref, o_ref, slot_scr, val_scr, *, n_exp, tb, cap):
    a = aff_ref[0]
    nrow = a.shape[0]
    bits = pltpu.bitcast(a, jnp.int32)
    gg = gg_ref[...]
    capf = float(cap)

    def group_total(mask_bf16):
        return jnp.sum(_dot(gg, mask_bf16), axis=1, keepdims=True)

    def bisect(_, carry):
        lo, hi = carry
        mid = lo + jnp.right_shift(hi - lo, 1)
        ok = group_total(jnp.where(bits >= mid, 1.0, 0.0).astype(BF16)) >= capf
        return jnp.where(ok, mid, lo), jnp.where(ok, hi, mid)

    lo0 = jnp.zeros((nrow, 1), jnp.int32)
    hi0 = jnp.full((nrow, 1), 0x7F800001, jnp.int32)
    thr, _ = lax.fori_loop(0, 32, bisect, (lo0, hi0))

    gt = bits > thr
    eq = bits == thr
    gtb = jnp.where(gt, 1.0, 0.0).astype(BF16)
    eqb = jnp.where(eq, 1.0, 0.0).astype(BF16)
    pg = _dot(gtb, u_ref[...])
    pe = _dot(eqb, u_ref[...])
    rg = jnp.broadcast_to(pg[:, LANES - 1:LANES], (nrow, LANES)).astype(BF16)
    re = jnp.broadcast_to(pe[:, LANES - 1:LANES], (nrow, LANES)).astype(BF16)
    off_g = _dot(sl_ref[...], rg)[:, 0:1]
    off_e = _dot(sl_ref[...], re)[:, 0:1]
    tot_g = _dot(gg, rg)[:, 0:1]
    need = capf - tot_g
    pos_e = off_e + pe - 1.0
    slot = jnp.where(gt, off_g + pg - 1.0, jnp.where(eq & (pos_e < need), tot_g + pos_e, -1.0))
    slot_scr[...] = slot

    lane = lax.broadcasted_iota(jnp.int32, (nrow, LANES), 1).astype(F32)
    tok = tb_ref[...] + lane
    tok_hi = jnp.floor(tok * (1.0 / 64.0))
    tok_lo = tok - 64.0 * tok_hi
    a1, a2, a3 = _split3(a)
    zero = jnp.zeros((nrow, LANES), F32)
    for c, arr in enumerate((tok_hi, tok_lo, a1.astype(F32), a2.astype(F32), a3.astype(F32), zero, zero, zero)):
        val_scr[:, c, :] = arr

    siota = lax.broadcasted_iota(jnp.int32, (cap, LANES), 0).astype(F32)
    lane8 = lax.broadcasted_iota(jnp.int32, (cap, SUBLANES), 1)
    for e in range(n_exp):
        def gather(t, acc):
            r = e * tb + t
            onehot = jnp.where(slot_scr[pl.ds(r, 1), :] == siota, 1.0, 0.0).astype(BF16)
            return acc + _dot_nt(onehot, val_scr[r].astype(BF16))

        acc = lax.fori_loop(0, tb, gather, jnp.zeros((cap, SUBLANES), F32))
        idx = acc[:, 0:1] * 64.0 + acc[:, 1:2]
        gate = acc[:, 2:3] + acc[:, 3:4] + acc[:, 4:5]
        o_ref[0, e] = jnp.where(lane8 == 0, idx, jnp.where(lane8 == 1, gate, 0.0))


def _select(aff, cap):
    bsz, n_exp, t_len = aff.shape
    tb = t_len // LANES
    nrow = n_exp * tb
    r = np.arange(nrow)
    same = (r[:, None] // tb) == (r[None, :] // tb)
    gg = jnp.asarray(same, BF16)
    sl = jnp.asarray(same & (r[None, :] < r[:, None]), BF16)
    ln = np.arange(LANES)
    u = jnp.asarray(ln[:, None] <= ln[None, :], BF16)
    tbase = jnp.asarray(((r % tb) * LANES).reshape(nrow, 1), F32)
    const = lambda shape: pl.BlockSpec(shape, lambda b: (0,) * len(shape))
    return pl.pallas_call(
        functools.partial(_select_body, n_exp=n_exp, tb=tb, cap=cap),
        grid=(bsz,),
        in_specs=[pl.BlockSpec((1, nrow, LANES), lambda b: (b, 0, 0)),
                  const((nrow, nrow)), const((nrow, nrow)), const((LANES, LANES)), const((nrow, 1))],
        out_specs=pl.BlockSpec((1, n_exp, cap, SUBLANES), lambda b: (b, 0, 0, 0)),
        out_shape=jax.ShapeDtypeStruct((bsz, n_exp, cap, SUBLANES), F32),
        scratch_shapes=[pltpu.VMEM((nrow, LANES), F32), pltpu.VMEM((nrow, SUBLANES, LANES), F32)],
        compiler_params=_cparams("parallel"),
        name="select",
    )(aff.reshape(bsz, nrow, LANES), gg, sl, u, tbase)


def _expert_body(idx_ref, idxn_ref, g_ref, mod_ref, wg_ref, wu_ref, wd_ref, h_hbm, x_hbm, o_hbm,
                 hs_scr, xs_scr, os_scr, sem_h, sem_x, sem_o, *, t_len, cap, rows, bsz, nsteps, conservative):
    b = pl.program_id(1)
    step = pl.program_id(0) * bsz + b
    nparts = cap // rows
    base = b * t_len
    base_next = ((b + 1) % bsz) * t_len
    gate_mod = mod_ref[0, 5:6, :]

    def start_gathers(iref, tok_base, s0, slot):
        for s in range(rows):
            tok = tok_base + iref[0, 0, s0 + s]
            pltpu.make_async_copy(h_hbm.at[pl.ds(tok, 1), :], hs_scr.at[slot, pl.ds(s, 1), :],
                                  sem_h.at[slot]).start()
            pltpu.make_async_copy(x_hbm.at[pl.ds(tok, 1), :], xs_scr.at[slot, pl.ds(s, 1), :],
                                  sem_x.at[slot]).start()

    def wait_gathers(slot):
        pltpu.make_async_copy(h_hbm.at[pl.ds(0, rows), :], hs_scr.at[slot], sem_h.at[slot]).wait()
        pltpu.make_async_copy(x_hbm.at[pl.ds(0, rows), :], xs_scr.at[slot], sem_x.at[slot]).wait()

    def wait_scatters(slot):
        pltpu.make_async_copy(os_scr.at[slot], o_hbm.at[pl.ds(0, rows), :], sem_o.at[slot]).wait()

    @pl.when(step == 0)
    def _():
        start_gathers(idx_ref, base, 0, 0)

    for part in range(nparts):
        g = step * nparts + part
        slot = part % 2 if nparts % 2 == 0 else g % 2
        nslot = 1 - slot
        s0 = part * rows
        wait_gathers(slot)
        if conservative:
            @pl.when(g >= 1)
            def _():
                wait_scatters(nslot)
        else:
            @pl.when(g >= 2)
            def _():
                wait_scatters(slot)
        if part + 1 < nparts:
            start_gathers(idx_ref, base, s0 + rows, nslot)
        else:
            start_gathers(idxn_ref, base_next, 0, nslot)

        hb = hs_scr[slot].astype(BF16)
        a = _dot(hb, wg_ref[...])
        up = _dot(hb, wu_ref[...])
        act = (a * _sigmoid(a)) * up
        y = _dot(act.astype(BF16), wd_ref[...]) * g_ref[0, s0:s0 + rows, :]
        os_scr[slot] = xs_scr[slot] + gate_mod * y
        for s in range(rows):
            tok = base + idx_ref[0, 0, s0 + s]
            pltpu.make_async_copy(os_scr.at[slot, pl.ds(s, 1), :], o_hbm.at[pl.ds(tok, 1), :],
                                  sem_o.at[slot]).start()

    @pl.when(step == nsteps - 1)
    def _():
        wait_gathers(nslot)
        wait_scatters(slot)
        if not conservative:
            wait_scatters(nslot)


def _experts(h2, x2, idx, gsel, mod, w_gate, w_up, w_down, layer, bsz, t_len):
    m, d = x2.shape
    n_exp, cap = idx.shape[1], idx.shape[2]
    ff = w_gate.shape[-1]
    rows = min(cap, 256)
    nparts = cap // rows
    nb = mod.shape[0]
    nsteps = n_exp * bsz
    assert bsz >= 2 and nsteps * nparts >= 2, "consecutive grid steps must touch different samples"
    conservative = nparts == 1 and bsz < 3
    idx_eb = idx.transpose(1, 0, 2).reshape(nsteps, 1, cap)
    g_eb = gsel.transpose(1, 0, 2).reshape(nsteps, cap, 1)
    any_spec = pl.BlockSpec(memory_space=pl.ANY)
    wspec = lambda r, c: pl.BlockSpec((None, None, r, c), lambda e, b: (layer, e, 0, 0))
    return pl.pallas_call(
        functools.partial(_expert_body, t_len=t_len, cap=cap, rows=rows, bsz=bsz, nsteps=nsteps,
                          conservative=conservative),
        grid=(n_exp, bsz),
        in_specs=[pl.BlockSpec((1, 1, cap), lambda e, b: (e * bsz + b, 0, 0), memory_space=pltpu.SMEM),
                  pl.BlockSpec((1, 1, cap), lambda e, b: ((e * bsz + b + 1) % nsteps, 0, 0),
                               memory_space=pltpu.SMEM),
                  pl.BlockSpec((1, cap, 1), lambda e, b: (e * bsz + b, 0, 0)),
                  pl.BlockSpec((1, 6, d), lambda e, b: (b if nb > 1 else 0, 0, 0)),
                  wspec(d, ff), wspec(d, ff), wspec(ff, d),
                  any_spec, any_spec],
        out_specs=any_spec,
        out_shape=jax.ShapeDtypeStruct((m, d), F32),
        scratch_shapes=[pltpu.VMEM((2, rows, d), F32), pltpu.VMEM((2, rows, d), F32), pltpu.VMEM((2, rows, d), F32),
                        pltpu.SemaphoreType.DMA((2,)), pltpu.SemaphoreType.DMA((2,)),
                        pltpu.SemaphoreType.DMA((2,))],
        input_output_aliases={8: 0},
        compiler_params=_cparams("arbitrary", "arbitrary"),
        name="experts",
    )(idx_eb, idx_eb, g_eb, mod, w_gate, w_up, w_down, h2, x2)


def _final_body(x_ref, w_ref, o_ref):
    x = x_ref[...]
    ms = jnp.mean(x * x, axis=-1, keepdims=True)
    o_ref[...] = (x * lax.rsqrt(ms + EPS)) * w_ref[...]


def _final_norm(x2, w):
    m, d = x2.shape
    tm = _pick(m, (512, 256, 128, 64))
    return pl.pallas_call(
        _final_body,
        grid=(m // tm,),
        in_specs=[pl.BlockSpec((tm, d), lambda i: (i, 0)), pl.BlockSpec((1, d), lambda i: (0, 0))],
        out_specs=pl.BlockSpec((tm, d), lambda i: (i, 0)),
        out_shape=jax.ShapeDtypeStruct((m, d), F32),
        compiler_params=_cparams("parallel"),
        name="final_norm",
    )(x2, w.reshape(1, d))


def _moe(x2, mod, lw, bsz, t_len):
    n_exp = lw["w_router_t"].shape[0]
    cap = (CAP_FACTOR * t_len) // n_exp
    h2, aff = _router(x2, mod, lw["w_router_t"], bsz, t_len)
    sel = _select(aff, cap)
    idx = sel[..., 0].astype(jnp.int32)
    return _experts(h2, x2, idx, sel[..., 1], mod, lw["w_gate"], lw["w_up"], lw["w_down"], lw["layer"], bsz, t_len)


def _mixer_front(x2, mod, lw, s0, bsz, t_len, heads, hb):
    p_main, p_ba = _inproj(x2, mod, lw["w_main"], lw["w_ba"], lw["layer"], t_len)
    gates = _gates(p_ba, lw["gparams"], heads)
    qkv = _shortconv(p_main, lw["conv_qkv"], bsz, t_len, heads)
    o_f, o_b, s_fin = _delta(qkv, gates, s0, bsz, t_len, heads, hb)
    return p_main, o_f, o_b, s_fin


def _mixer_back(x2, mod, lw, p_main, o_f, o_b, bsz, t_len, heads, on_grid):
    cw = lw["conf_dw"].shape[1]
    nblk = cw // LANES
    if on_grid:
        y = _confconv(p_main, 4 * heads * LANES, lw["conf_dw"], lw["conf_dw_b"], bsz, t_len, GRID_W, nblk // 2)
    else:
        y = _confconv(p_main, 4 * heads * LANES, lw["conf_dw"], lw["conf_dw_b"], bsz, t_len, t_len, nblk)
    cat = _mixfin(o_f, o_b, p_main, y, lw["dn_norm"], lw["conf_ln_g"], lw["conf_ln_b"], heads)
    return _outproj(cat, lw["w_out"], lw["layer"], x2, mod, t_len)


def kernel(x, c, ctx, c_ctx, w_ada, b_ada, w_in, conv_qkv, a_log, dt_bias, dn_norm, conf_dw, conf_dw_b,
           conf_ln_g, conf_ln_b, w_out, w_router, w_gate_e, w_up_e, w_down_e, final_norm):
    bsz, seq, d = x.shape
    ctx_len = ctx.shape[1]
    depth = w_ada.shape[0]
    dh = dn_norm.shape[-1]
    dnw = conv_qkv.shape[-1] // 3
    heads = dnw // dh
    cw = conf_dw.shape[-1]
    conf_off = w_in.shape[-1] - 2 * cw
    assert dh == LANES and 4 * heads <= LANES and conf_off == 4 * dnw + 4 * heads
    hb = min(16, heads)

    nb = -(-(bsz + 1) // SUBLANES) * SUBLANES
    cond = jnp.zeros((nb, d), F32).at[:bsz].set(c).at[bsz].set(c_ctx)
    mods = _adaln(cond, w_ada, b_ada).reshape(depth, nb, 6, d)

    w_main = jnp.concatenate([w_in[:, :, :4 * dnw], w_in[:, :, conf_off:]], axis=-1).astype(BF16)
    w_ba = jnp.pad(w_in[:, :, 4 * dnw:conf_off], ((0, 0), (0, 0), (0, LANES - 4 * heads))).astype(BF16)
    w_out_b = w_out.astype(BF16)
    w_gate_b = w_gate_e.astype(BF16)
    w_up_b = w_up_e.astype(BF16)
    w_down_b = w_down_e.astype(BF16)
    w_router_t = jnp.swapaxes(w_router, 1, 2)
    gparams = jnp.zeros((depth, 2, LANES), F32)
    gparams = gparams.at[:, 0, 2 * heads:4 * heads].set(a_log.reshape(depth, 2 * heads))
    gparams = gparams.at[:, 1, 2 * heads:4 * heads].set(dt_bias.reshape(depth, 2 * heads))

    x_lat = x.reshape(bsz * seq, d)
    x_ctx = ctx.reshape(bsz * ctx_len, d)
    zero_state = jnp.zeros((2, bsz, heads, dh, dh), F32)
    for l in range(depth):
        last = l == depth - 1
        lw = dict(layer=l, w_main=w_main, w_ba=w_ba, w_out=w_out_b, w_gate=w_gate_b, w_up=w_up_b, w_down=w_down_b,
                  gparams=gparams[l], conv_qkv=conv_qkv[l], dn_norm=dn_norm[l], conf_dw=conf_dw[l],
                  conf_dw_b=conf_dw_b[l], conf_ln_g=conf_ln_g[l], conf_ln_b=conf_ln_b[l], w_router_t=w_router_t[l])
        ml = mods[l, :bsz]
        mc = mods[l, bsz:bsz + 1]
        pc, ofc, obc, ctx_states = _mixer_front(x_ctx, mc, lw, zero_state, bsz, ctx_len, heads, hb)
        pl_, ofl, obl, _ = _mixer_front(x_lat, ml, lw, ctx_states, bsz, seq, heads, hb)
        x_lat = _mixer_back(x_lat, ml, lw, pl_, ofl, obl, bsz, seq, heads, True)
        x_lat = _moe(x_lat, ml, lw, bsz, seq)
        if not last:
            x_ctx = _mixer_back(x_ctx, mc, lw, pc, ofc, obc, bsz, ctx_len, heads, False)
            x_ctx = _moe(x_ctx, mc, lw, bsz, ctx_len)
    return _final_norm(x_lat, final_norm).reshape(bsz, seq, d)
```
